```python
import jax
import jax.numpy as jnp
from jax import lax
import numpy as np

D_MODEL = 4096
BATCH = 4
SEQ = 2048
DEPTH = 2
DEC_BATCH = 8
DEC_SEQ = 4
PAST_LEN = 16384
PAGE_SIZE = 128

EPS = 1e-6
NEG = -1e30
ROPE_THETA = 10000.0
N_BRANCH = 4
BR_W = D_MODEL // N_BRANCH
ATT_HEADS = 8
ATT_DH = BR_W // ATT_HEADS
MOBA_BLOCK = 256
MOBA_TOPK = 3
MOBA_QB = 16
CONV_C = BR_W
CONV_K = 31
HG_HEADS = 8
HG_DK = BR_W // HG_HEADS
HG_DV = HG_DK
HG_CHUNK = 64
X_HEADS = 4
X_DH = BR_W // X_HEADS
N_MEM = 256
GATE_RANK = 512
SPLIT_WIDTHS = (BR_W, BR_W, BR_W, 2 * CONV_C, BR_W, BR_W, BR_W, BR_W, BR_W, GATE_RANK)
IN_W = 10 * BR_W + GATE_RANK
F_DENSE = 256 * ((8 * D_MODEL // 3 + 255) // 256)
F_EXPERT = F_DENSE // 2
N_EXPERTS = 8
TOP_K = 2
MOE_BLOCK = 128

kernel_name = 'hybrid_moba_conformer_hgrn2_decode_step'

F32 = jnp.float32


def rms_norm(x, g):
    xf = x.astype(F32)
    y = xf * lax.rsqrt(jnp.mean(xf * xf, axis=-1, keepdims=True) + EPS)
    return (y * g.astype(F32)).astype(x.dtype)


def layer_norm(x, g, b):
    xf = x.astype(F32)
    mu = jnp.mean(xf, axis=-1, keepdims=True)
    var = jnp.mean(jnp.square(xf - mu), axis=-1, keepdims=True)
    y = (xf - mu) * lax.rsqrt(var + EPS)
    return (y * g.astype(F32) + b.astype(F32)).astype(x.dtype)


def rope(x, pos):
    half = x.shape[-1] // 2
    inv = ROPE_THETA ** (-jnp.arange(half, dtype=F32) / half)
    ang = pos.astype(F32)[:, None] * inv[None, :]
    cos = jnp.cos(ang)[None, :, None, :]
    sin = jnp.sin(ang)[None, :, None, :]
    x1 = x[..., :half].astype(F32)
    x2 = x[..., half:].astype(F32)
    return jnp.concatenate([x1 * cos - x2 * sin, x2 * cos + x1 * sin], axis=-1).astype(x.dtype)


def moba_attention(q, k_all, v_all, pos0):
    B, Tq, H, dh = q.shape
    Tk = k_all.shape[1]
    n_blk = -(-Tk // MOBA_BLOCK)
    pad = n_blk * MOBA_BLOCK - Tk

    def to_blocks(x):
        x = jnp.pad(x, ((0, 0), (0, pad), (0, 0), (0, 0)))
        return x.reshape(B, n_blk, MOBA_BLOCK, H, dh).transpose(0, 3, 1, 2, 4)

    kb = to_blocks(k_all)
    vb = to_blocks(v_all)
    k_mean = jnp.mean(kb.astype(F32), axis=3)
    q_pos = pos0 + jnp.arange(Tq, dtype=jnp.int32)
    own = q_pos // MOBA_BLOCK
    qh = q.transpose(0, 2, 1, 3)
    gscore = jnp.einsum('bhqd,bhnd->bhqn', qh.astype(F32), k_mean)
    fully_past = jnp.arange(n_blk)[None, :] < own[:, None]
    gscore = jnp.where(fully_past[None, None], gscore, NEG)
    n_sel = min(MOBA_TOPK, n_blk)
    _, top_i = lax.top_k(gscore, n_sel)
    top_i = top_i.astype(jnp.int32)
    ok_sel = top_i < own[None, None, :, None]
    idx = jnp.concatenate(
        [top_i, jnp.broadcast_to(own[None, None, :, None], (B, H, Tq, 1)).astype(jnp.int32)], axis=-1)
    ok = jnp.concatenate([ok_sel, jnp.ones((B, H, Tq, 1), bool)], axis=-1)
    is_own = jnp.arange(n_sel + 1) == n_sel
    qb = MOBA_QB if Tq % MOBA_QB == 0 else Tq
    n_c = Tq // qb

    def chunk(x):
        return jnp.moveaxis(x.reshape(B, H, n_c, qb, *x.shape[3:]), 2, 0)

    gather = jax.vmap(jax.vmap(lambda blocks, ix: blocks[ix]))
    scale = dh ** -0.5

    def attend(args):
        q_c, idx_c, ok_c, pos_c = args
        kg = gather(kb, idx_c)
        vg = gather(vb, idx_c)
        s = jnp.einsum('bhqd,bhqnld->bhqnl', q_c, kg).astype(F32) * scale
        key_pos = idx_c[..., None] * MOBA_BLOCK + jnp.arange(MOBA_BLOCK)
        causal = key_pos <= pos_c[None, None, :, None, None]
        mask = jnp.where(is_own[:, None], causal, ok_c[..., None])
        s = jnp.where(mask, s, NEG)
        p = jax.nn.softmax(s.reshape(B, H, qb, -1), axis=-1).reshape(s.shape).astype(v_all.dtype)
        return jnp.einsum('bhqnl,bhqnld->bhqd', p, vg)

    out = lax.map(attend, (chunk(qh), chunk(idx), chunk(ok), q_pos.reshape(n_c, qb)))
    out = jnp.moveaxis(out, 0, 2).reshape(B, H, Tq, dh)
    return out.transpose(0, 2, 1, 3).reshape(B, Tq, H * dh)


def conformer_conv(cin, buf, w, b, ln_g, ln_b):
    a, g = jnp.split(cin, 2, axis=-1)
    u = a * jax.nn.sigmoid(g)
    ext = jnp.concatenate([buf.astype(u.dtype), u], axis=1)
    y = lax.conv_general_dilated(ext, w.astype(ext.dtype)[:, None, :], window_strides=(1,), padding='VALID',
                                 dimension_numbers=('NWC', 'WIO', 'NWC'), feature_group_count=CONV_C)
    y = layer_norm(y + b.astype(y.dtype), ln_g, ln_b)
    return jax.nn.silu(y), ext[:, -(CONV_K - 1):]


def chunked_gla(q, k, v, log_f, s0):
    B, T, H, dk = q.shape
    dv = v.shape[-1]
    c = HG_CHUNK if T % HG_CHUNK == 0 else T
    n = T // c

    def chunks(x):
        return x.reshape(B, n, c, H, x.shape[-1]).transpose(1, 0, 3, 2, 4)

    tri = jnp.tril(jnp.ones((c, c), bool))[:, :, None]

    def step(S, xs):
        qc, kc, vc, gc = xs
        bcum = jnp.cumsum(gc, axis=2)
        diff = jnp.where(tri, bcum[:, :, :, None, :] - bcum[:, :, None, :, :], 0.0)
        decay = jnp.where(tri, jnp.exp(diff), 0.0)
        attn = jnp.einsum('bhtd,bhsd,bhtsd->bhts', qc, kc, decay)
        o = jnp.einsum('bhts,bhsv->bhtv', attn, vc) + jnp.einsum('bhtd,bhdv->bhtv', qc * jnp.exp(bcum), S)
        b_last = bcum[:, :, -1:, :]
        S = jnp.exp(b_last[:, :, 0])[..., None] * S + jnp.einsum('bhsd,bhsv->bhdv', kc * jnp.exp(b_last - bcum), vc)
        return S, o

    S, o = lax.scan(step, s0, (chunks(q), chunks(k), chunks(v), chunks(log_f)))
    return o.transpose(1, 0, 3, 2, 4).reshape(B, T, H, dv), S


def hgrn2(hf, hi, hq, hog, s0, lb, norm_g):
    B, T, _ = hf.shape
    shp = (B, T, HG_HEADS, HG_DK)
    z = hf.astype(F32).reshape(shp)
    lbh = lb.astype(F32).reshape(HG_HEADS, HG_DK)
    f = lbh + (1.0 - lbh) * jax.nn.sigmoid(z)
    log_f = jnp.log(f)
    k = 1.0 - f
    q = jax.nn.silu(hq.astype(F32)).reshape(shp)
    v = hi.astype(F32).reshape(B, T, HG_HEADS, HG_DV)
    o, s_new = chunked_gla(q, k, v, log_f, s0.astype(F32))
    o = rms_norm(o, norm_g.reshape(HG_HEADS, HG_DV)).reshape(B, T, -1).astype(hf.dtype)
    return o * jax.nn.silu(hog), s_new


def memory_kv(mem, norm_g, w_kv, k_norm_g):
    B, M, _ = mem.shape
    kv = jnp.einsum('bmd,de->bme', rms_norm(mem, norm_g), w_kv)
    k, v = jnp.split(kv, 2, axis=-1)
    k = rms_norm(k.reshape(B, M, X_HEADS, X_DH), k_norm_g)
    return k, v.reshape(B, M, X_HEADS, X_DH)


def cross_attention(q, mem_k, mem_v):
    B, T = q.shape[:2]
    s = jnp.einsum('bthd,bmhd->bhtm', q, mem_k).astype(F32) * (X_DH ** -0.5)
    p = jax.nn.softmax(s, axis=-1).astype(mem_v.dtype)
    return jnp.einsum('bhtm,bmhd->bthd', p, mem_v).reshape(B, T, X_HEADS * X_DH)


def swiglu(x, w1, w3, w2):
    return (jax.nn.silu(x @ w1) * (x @ w3)) @ w2


def moe_swiglu(h, w_router, w1, w3, w2):
    T, D = h.shape
    A = T * TOP_K
    logits = jnp.einsum('td,de->te', h.astype(F32), w_router.astype(F32))
    top_logit, top_e = lax.top_k(logits, TOP_K)
    gate = jax.nn.softmax(top_logit, axis=-1).astype(h.dtype)
    e_flat = top_e.reshape(A)
    tok_flat = jnp.repeat(jnp.arange(T, dtype=jnp.int32), TOP_K)
    g_flat = gate.reshape(A)
    order = jnp.argsort(e_flat)
    e_s, tok_s, g_s = e_flat[order], tok_flat[order], g_flat[order]
    counts = jnp.bincount(e_flat, length=N_EXPERTS)
    blocks_per_e = (counts + MOE_BLOCK - 1) // MOE_BLOCK
    blk_end = jnp.cumsum(blocks_per_e)
    blk_start = blk_end - blocks_per_e
    row_start = jnp.cumsum(counts) - counts
    dest = blk_start[e_s] * MOE_BLOCK + (jnp.arange(A) - row_start[e_s])
    n_blocks = (A + MOE_BLOCK - 1) // MOE_BLOCK + N_EXPERTS
    rows_tok = jnp.zeros(n_blocks * MOE_BLOCK, jnp.int32).at[dest].set(tok_s)
    rows_gate = jnp.zeros(n_blocks * MOE_BLOCK, h.dtype).at[dest].set(g_s)
    blk_e = jnp.minimum(jnp.searchsorted(blk_end, jnp.arange(n_blocks), side='right'), N_EXPERTS - 1)
    xs = h[rows_tok].reshape(n_blocks, MOE_BLOCK, D)

    def expert_block(args):
        xb, e = args
        return swiglu(xb, w1[e], w3[e], w2[e])

    ys = lax.map(expert_block, (xs, blk_e)).reshape(-1, D)
    return jax.ops.segment_sum(ys * rows_gate[:, None], rows_tok, num_segments=T)


def token_mix(h, pos0, k_past, v_past, mem_k, mem_v, conv_buf, s0,
              w_in, q_norm_g, k_norm_g, conv_w, conv_b, conv_ln_g, conv_ln_b,
              lb, hg_norm_g, xq_norm_g, w_gate, w_branch, w_out):
    B, T, _ = h.shape
    pos = pos0 + jnp.arange(T, dtype=jnp.int32)
    proj = jnp.einsum('btd,de->bte', h, w_in)
    cuts, acc = [], 0
    for wdt in SPLIT_WIDTHS[:-1]:
        acc += wdt
        cuts.append(acc)
    aq, ak, av, cin, hf, hi, hq, hog, xq, gz = jnp.split(proj, cuts, axis=-1)
    q = rope(rms_norm(aq.reshape(B, T, ATT_HEADS, ATT_DH), q_norm_g), pos)
    k = rope(rms_norm(ak.reshape(B, T, ATT_HEADS, ATT_DH), k_norm_g), pos)
    v = av.reshape(B, T, ATT_HEADS, ATT_DH)
    o_att = moba_attention(q, jnp.concatenate([k_past.astype(k.dtype), k], axis=1),
                           jnp.concatenate([v_past.astype(v.dtype), v], axis=1), pos0)
    o_conv, conv_new = conformer_conv(cin, conv_buf, conv_w, conv_b, conv_ln_g, conv_ln_b)
    o_hg, s_new = hgrn2(hf, hi, hq, hog, s0, lb, hg_norm_g)
    o_x = cross_attention(rms_norm(xq.reshape(B, T, X_HEADS, X_DH), xq_norm_g), mem_k, mem_v)
    merged = None
    for bi, o_b in enumerate((o_att, o_conv, o_hg, o_x)):
        term = jax.nn.sigmoid(gz @ w_gate[bi]) * (o_b @ w_branch[bi])
        merged = term if merged is None else merged + term
    return merged @ w_out, k, v, conv_new, s_new


def setup_inputs(seed: int = 0) -> dict:
    key = jax.random.key(seed)
    ks = iter(jax.random.split(key, 48))

    def nrm(shape, scale=1.0):
        return jax.random.normal(next(ks), shape, F32) * scale

    def gain(shape):
        return 1.0 + nrm(shape, 0.02)

    n_pages = PAST_LEN // PAGE_SIZE
    n_used = DEC_BATCH * n_pages
    n_pool = n_used + n_used // 4
    page_table = jax.random.permutation(next(ks), n_pool)[:n_used].reshape(DEC_BATCH, n_pages).astype(jnp.int32)
    n_dense = (DEPTH + 1) // 2
    n_moe = DEPTH // 2
    return {
        'x_prompt': nrm((BATCH, SEQ, D_MODEL)),
        'x_sample': nrm((DEC_BATCH, DEC_SEQ, D_MODEL)),
        'cache_k': nrm((DEPTH, n_pool, PAGE_SIZE, ATT_HEADS, ATT_DH)),
        'cache_v': nrm((DEPTH, n_pool, PAGE_SIZE, ATT_HEADS, ATT_DH)),
        'page_table': page_table,
        'cache_mem_k': nrm((DEPTH, DEC_BATCH, N_MEM, X_HEADS, X_DH)),
        'cache_mem_v': nrm((DEPTH, DEC_BATCH, N_MEM, X_HEADS, X_DH)),
        'state_conv': nrm((DEPTH, DEC_BATCH, CONV_K - 1, CONV_C), 0.5),
        'state_hgrn': nrm((DEPTH, DEC_BATCH, HG_HEADS, HG_DK, HG_DV), 0.5),
        'mem_prompt': nrm((BATCH, N_MEM, D_MODEL)),
        'norm_mix_g': gain((DEPTH, D_MODEL)),
        'norm_mem_g': gain((DEPTH, D_MODEL)),
        'norm_ffn_g': gain((DEPTH, D_MODEL)),
        'w_in': nrm((DEPTH, D_MODEL, IN_W), D_MODEL ** -0.5),
        'attn_q_norm_g': gain((DEPTH, ATT_DH)),
        'attn_k_norm_g': gain((DEPTH, ATT_DH)),
        'conv_w': nrm((DEPTH, CONV_K, CONV_C), CONV_K ** -0.5),
        'conv_b': nrm((DEPTH, CONV_C), 0.02),
        'conv_ln_g': gain((DEPTH, CONV_C)),
        'conv_ln_b': nrm((DEPTH, CONV_C), 0.02),
        'hgrn_lb': nrm((DEPTH, BR_W), 0.5),
        'hgrn_norm_g': gain((DEPTH, BR_W)),
        'w_mem_kv': nrm((DEPTH, D_MODEL, 2 * BR_W), D_MODEL ** -0.5),
        'xq_norm_g': gain((DEPTH, X_DH)),
        'xk_norm_g': gain((DEPTH, X_DH)),
        'w_gate': nrm((DEPTH, N_BRANCH, GATE_RANK, D_MODEL), GATE_RANK ** -0.5),
        'w_branch': nrm((DEPTH, N_BRANCH, BR_W, D_MODEL), BR_W ** -0.5),
        'w_out': nrm((DEPTH, D_MODEL, D_MODEL), D_MODEL ** -0.5),
        'dense_w1': nrm((n_dense, D_MODEL, F_DENSE), D_MODEL ** -0.5),
        'dense_w3': nrm((n_dense, D_MODEL, F_DENSE), D_MODEL ** -0.5),
        'dense_w2': nrm((n_dense, F_DENSE, D_MODEL), F_DENSE ** -0.5),
        'moe_router': nrm((n_moe, D_MODEL, N_EXPERTS), D_MODEL ** -0.5),
        'moe_w1': nrm((n_moe, N_EXPERTS, D_MODEL, F_EXPERT), D_MODEL ** -0.5),
        'moe_w3': nrm((n_moe, N_EXPERTS, D_MODEL, F_EXPERT), D_MODEL ** -0.5),
        'moe_w2': nrm((n_moe, N_EXPERTS, F_EXPERT, D_MODEL), F_EXPERT ** -0.5),
    }


def reference(x_prompt, x_sample, cache_k, cache_v, page_table, cache_mem_k, cache_mem_v,
              state_conv, state_hgrn, mem_prompt, norm_mix_g, norm_mem_g, norm_ffn_g, w_in,
              attn_q_norm_g, attn_k_norm_g, conv_w, conv_b, conv_ln_g, conv_ln_b, hgrn_lb,
              hgrn_norm_g, w_mem_kv, xq_norm_g, xk_norm_g, w_gate, w_branch, w_out,
              dense_w1, dense_w3, dense_w2, moe_router, moe_w1, moe_w3, moe_w2):
    bp = x_prompt.shape[0]
    bs = x_sample.shape[0]
    n_pages = page_table.shape[1]
    past_len = n_pages * cache_k.shape[2]
    p_lb = jax.nn.softmax(hgrn_lb.astype(F32), axis=0)
    lower_bounds = jnp.concatenate([jnp.zeros_like(p_lb[:1]), jnp.cumsum(p_lb[1:], axis=0)], axis=0)
    xp, xs = x_prompt, x_sample
    kp_l, vp_l, ks_l, vs_l, mkp_l, mvp_l, cp_l, cs_l, sp_l, ss_l = [], [], [], [], [], [], [], [], [], []
    for l in range(DEPTH):
        mix_w = (w_in[l], attn_q_norm_g[l], attn_k_norm_g[l], conv_w[l], conv_b[l], conv_ln_g[l], conv_ln_b[l],
                 lower_bounds[l], hgrn_norm_g[l], xq_norm_g[l], w_gate[l], w_branch[l], w_out[l])
        mk_p, mv_p = memory_kv(mem_prompt, norm_mem_g[l], w_mem_kv[l], xk_norm_g[l])
        empty = jnp.zeros((bp, 0, ATT_HEADS, ATT_DH), x_prompt.dtype)
        y, k_new, v_new, c_new, s_new = token_mix(
            rms_norm(xp, norm_mix_g[l]), 0, empty, empty, mk_p, mv_p,
            jnp.zeros((bp, CONV_K - 1, CONV_C), x_prompt.dtype),
            jnp.zeros((bp, HG_HEADS, HG_DK, HG_DV), F32), *mix_w)
        xp = xp + y
        kp_l.append(k_new); vp_l.append(v_new); mkp_l.append(mk_p); mvp_l.append(mv_p)
        cp_l.append(c_new); sp_l.append(s_new)
        k_past = cache_k[l, page_table].reshape(bs, past_len, ATT_HEADS, ATT_DH)
        v_past = cache_v[l, page_table].reshape(bs, past_len, ATT_HEADS, ATT_DH)
        y, k_new, v_new, c_new, s_new = token_mix(
            rms_norm(xs, norm_mix_g[l]), past_len, k_past, v_past, cache_mem_k[l], cache_mem_v[l],
            state_conv[l], state_hgrn[l], *mix_w)
        xs = xs + y
        ks_l.append(k_new); vs_l.append(v_new); cs_l.append(c_new); ss_l.append(s_new)
        hp = rms_norm(xp, norm_ffn_g[l])
        hs = rms_norm(xs, norm_ffn_g[l])
        i = l // 2
        if l % 2 == 0:
            xp = xp + swiglu(hp, dense_w1[i], dense_w3[i], dense_w2[i])
            xs = xs + swiglu(hs, dense_w1[i], dense_w3[i], dense_w2[i])
        else:
            xp = xp + moe_swiglu(hp.reshape(-1, D_MODEL), moe_router[i], moe_w1[i], moe_w3[i],
                                 moe_w2[i]).reshape(xp.shape)
            xs = xs + moe_swiglu(hs.reshape(-1, D_MODEL), moe_router[i], moe_w1[i], moe_w3[i],
                                 moe_w2[i]).reshape(xs.shape)
    return (xp, xs, jnp.stack(kp_l), jnp.stack(vp_l), jnp.stack(ks_l), jnp.stack(vs_l),
            jnp.stack(mkp_l), jnp.stack(mvp_l), jnp.stack(cp_l), jnp.stack(cs_l),
            jnp.stack(sp_l), jnp.stack(ss_l))
```

```python
import functools

import jax
import jax.numpy as jnp
from jax import lax
from jax.experimental import pallas as pl
from jax.experimental.pallas import tpu as pltpu

F32 = jnp.float32
BF16 = jnp.bfloat16
I32 = jnp.int32
HI = lax.Precision.HIGHEST

EPS = 1e-6
NEG = -1e30
ROPE_THETA = 10000.0
ATT_HEADS = 8
ATT_DH = 128
MOBA_BLOCK = 256
MOBA_TOPK = 3
CONV_K = 31
CONV_PAD = 32
HG_HEADS = 8
HG_DK = 128
HG_CHUNK = 64
X_HEADS = 4
X_DH = 256
GATE_RANK = 512
N_EXPERTS = 8
TOP_K = 2
BR_W = 1024

V7X_VMEM_BYTES = 64 * 1024 * 1024
VMEM_CAP = V7X_VMEM_BYTES - 8 * 1024 * 1024
LANE = 128
SUBLANE = 8

NT = (((1,), (1,)), ((), ()))
TN = (((0,), (0,)), ((), ()))


def _params(sem, est_bytes):
    limit = int(min(max(2 * est_bytes + (8 << 20), 32 << 20), VMEM_CAP))
    return pltpu.CompilerParams(dimension_semantics=sem, vmem_limit_bytes=limit)


def _tile(n, pref, unit):
    if n <= pref:
        return n
    t = (pref // unit) * unit
    while t > unit and n % t:
        t -= unit
    assert n % t == 0, (n, pref, unit)
    return t


def _nbytes(shape, dtype):
    n = 1
    for s in shape:
        n *= s
    return n * jnp.dtype(dtype).itemsize


def _sigmoid(x):
    return 1.0 / (1.0 + jnp.exp(-x))


def _silu(x):
    return x * _sigmoid(x)


def _mm_body(a_ref, b_ref, *rest, has_res):
    if has_res:
        r_ref, o_ref = rest
    else:
        (o_ref,) = rest
    acc = jnp.dot(a_ref[...], b_ref[...], preferred_element_type=F32)
    if has_res:
        acc = acc + r_ref[...]
    o_ref[...] = acc.astype(o_ref.dtype)


def _matmul(a, b, res=None, *, out_dtype=F32, tm=1024, tn=512):
    M, K = a.shape
    N = b.shape[1]
    tm = _tile(M, tm, 16)
    tn = _tile(N, tn, LANE)
    in_specs = [pl.BlockSpec((tm, K), lambda i, j: (i, 0)),
                pl.BlockSpec((K, tn), lambda i, j: (0, j))]
    args = [a, b]
    est = _nbytes((tm, K), a.dtype) + _nbytes((K, tn), b.dtype) + 2 * _nbytes((tm, tn), F32)
    if res is not None:
        in_specs.append(pl.BlockSpec((tm, tn), lambda i, j: (i, j)))
        args.append(res)
        est += _nbytes((tm, tn), F32)
    return pl.pallas_call(
        functools.partial(_mm_body, has_res=res is not None),
        grid=(M // tm, N // tn),
        in_specs=in_specs,
        out_specs=pl.BlockSpec((tm, tn), lambda i, j: (i, j)),
        out_shape=jax.ShapeDtypeStruct((M, N), out_dtype),
        compiler_params=_params(("parallel", "arbitrary"), est),
    )(*args)


def _rms_body(x_ref, g_ref, *rest, with_router, keep_f32):
    x = x_ref[...]
    h = x * lax.rsqrt(jnp.mean(x * x, axis=-1, keepdims=True) + EPS) * g_ref[...]
    if with_router:
        wr_ref, o_ref, lg_ref = rest
        lg_ref[...] = jnp.dot(h, wr_ref[...], precision=HI, preferred_element_type=F32)
    else:
        (o_ref,) = rest
    o_ref[...] = h if keep_f32 else h.astype(o_ref.dtype)


def _rmsnorm(x, g, w_router=None, *, keep_f32=False, tm=256):
    M, D = x.shape
    tm = _tile(M, tm, 16)
    in_specs = [pl.BlockSpec((tm, D), lambda i: (i, 0)), pl.BlockSpec((1, D), lambda i: (0, 0))]
    args = [x, g.reshape(1, D)]
    odt = F32 if keep_f32 else BF16
    out_shape = [jax.ShapeDtypeStruct((M, D), odt)]
    out_specs = [pl.BlockSpec((tm, D), lambda i: (i, 0))]
    est = 2 * _nbytes((tm, D), F32) + _nbytes((tm, D), odt)
    if w_router is not None:
        wr = jnp.pad(w_router.astype(F32), ((0, 0), (0, LANE - w_router.shape[1])))
        in_specs.append(pl.BlockSpec((D, LANE), lambda i: (0, 0)))
        args.append(wr)
        out_shape.append(jax.ShapeDtypeStruct((M, LANE), F32))
        out_specs.append(pl.BlockSpec((tm, LANE), lambda i: (i, 0)))
        est += _nbytes((D, LANE), F32)
    outs = pl.pallas_call(
        functools.partial(_rms_body, with_router=w_router is not None, keep_f32=keep_f32),
        grid=(M // tm,),
        in_specs=in_specs,
        out_specs=out_specs,
        out_shape=out_shape,
        compiler_params=_params(("parallel",), est),
    )(*args)
    if w_router is not None:
        return outs[0], outs[1][:, :w_router.shape[1]]
    return outs[0]


def _headnorm_body(x_ref, g_ref, o_ref, *, heads, dh):
    g = g_ref[...]
    for h in range(heads):
        x = x_ref[:, h * dh:(h + 1) * dh]
        o_ref[:, h * dh:(h + 1) * dh] = x * lax.rsqrt(jnp.mean(x * x, axis=-1, keepdims=True) + EPS) * g


def _headnorm(kv, g, *, heads, dh, tm=256):
    M = kv.shape[0]
    W = heads * dh
    tm = _tile(M, tm, SUBLANE)
    return pl.pallas_call(
        functools.partial(_headnorm_body, heads=heads, dh=dh),
        grid=(M // tm,),
        in_specs=[pl.BlockSpec((tm, W), lambda i: (i, 0)), pl.BlockSpec((1, dh), lambda i: (0, 0))],
        out_specs=pl.BlockSpec((tm, W), lambda i: (i, 0)),
        out_shape=jax.ShapeDtypeStruct((M, W), F32),
        compiler_params=_params(("parallel",), 2 * _nbytes((tm, W), F32)),
    )(kv, g.reshape(1, dh))


def _qk_body(q_ref, k_ref, cos_ref, sin_ref, gq_ref, gk_ref, qo_ref, ko_ref):
    cos = cos_ref[...]
    sin = sin_ref[...]
    for src, g_ref, dst in ((q_ref, gq_ref, qo_ref), (k_ref, gk_ref, ko_ref)):
        g = g_ref[...]
        for h in range(ATT_HEADS):
            x = src[:, h * ATT_DH:(h + 1) * ATT_DH]
            y = x * lax.rsqrt(jnp.mean(x * x, axis=-1, keepdims=True) + EPS) * g
            dst[:, h * ATT_DH:(h + 1) * ATT_DH] = y * cos + pltpu.roll(y, ATT_DH // 2, 1) * sin


def _qk_prep(proj, cos, sin, gq, gk, *, tt=256):
    M = proj.shape[0]
    P = cos.shape[0]
    tt = _tile(P, tt, SUBLANE)
    npos = P // tt
    W = ATT_HEADS * ATT_DH
    return pl.pallas_call(
        _qk_body,
        grid=(M // tt,),
        in_specs=[pl.BlockSpec((tt, W), lambda i: (i, 0)),
                  pl.BlockSpec((tt, W), lambda i: (i, 1)),
                  pl.BlockSpec((tt, ATT_DH), lambda i: (i % npos, 0)),
                  pl.BlockSpec((tt, ATT_DH), lambda i: (i % npos, 0)),
                  pl.BlockSpec((1, ATT_DH), lambda i: (0, 0)),
                  pl.BlockSpec((1, ATT_DH), lambda i: (0, 0))],
        out_specs=[pl.BlockSpec((tt, W), lambda i: (i, 0)), pl.BlockSpec((tt, W), lambda i: (i, 0))],
        out_shape=[jax.ShapeDtypeStruct((M, W), F32), jax.ShapeDtypeStruct((M, W), F32)],
        compiler_params=_params(("parallel",), 4 * _nbytes((tt, W), F32)),
    )(proj, proj, cos, sin, gq.reshape(1, ATT_DH), gk.reshape(1, ATT_DH))


def _rope_tables(pos):
    half = ATT_DH // 2
    inv = ROPE_THETA ** (-jnp.arange(half, dtype=F32) / half)
    ang = pos.astype(F32)[:, None] * inv[None, :]
    cos = jnp.cos(ang)
    sin = jnp.sin(ang)
    return jnp.concatenate([cos, cos], axis=-1), jnp.concatenate([-sin, sin], axis=-1)


def _moba_prompt_body(q_ref, k_ref, v_ref, o_ref, *, nblk):
    i = pl.program_id(2)
    q = q_ref[0]
    k = k_ref[0]
    v = v_ref[0]
    L = q.shape[0]
    T = k.shape[0]
    kmean = jnp.mean(k.reshape(nblk, L, ATT_DH), axis=1)
    gs = lax.dot_general(q, kmean, NT, precision=HI, preferred_element_type=F32)
    jcol = lax.broadcasted_iota(I32, gs.shape, 1)
    past = jcol < i
    gs = jnp.where(past, gs, NEG)
    rank = jnp.zeros(gs.shape, F32)
    for jp in range(nblk):
        c = gs[:, jp:jp + 1]
        tie = jnp.where(jcol > jp, 1.0, 0.0)
        rank = rank + jnp.where(c > gs, 1.0, jnp.where(c == gs, tie, 0.0))
    sel = jnp.where(past, jnp.where(rank < MOBA_TOPK, 1.0, 0.0), 0.0)
    kblk = lax.broadcasted_iota(I32, (nblk, T), 1) // L
    expand = jnp.where(kblk == lax.broadcasted_iota(I32, (nblk, T), 0), 1.0, 0.0)
    picked = jnp.dot(sel, expand, preferred_element_type=F32)
    key = lax.broadcasted_iota(I32, (L, T), 1)
    qpos = i * L + lax.broadcasted_iota(I32, (L, T), 0)
    allowed = jnp.where(key >= i * L, jnp.where(key <= qpos, 1.0, 0.0), picked)
    s = lax.dot_general(q.astype(BF16), k.astype(BF16), NT, preferred_element_type=F32) * (ATT_DH ** -0.5)
    s = jnp.where(allowed > 0.5, s, NEG)
    p = jnp.exp(s - jnp.max(s, axis=-1, keepdims=True))
    l = jnp.sum(p, axis=-1, keepdims=True)
    o = jnp.dot(p.astype(BF16), v.astype(BF16), preferred_element_type=F32)
    o_ref[0] = (o / l).astype(o_ref.dtype)


def _moba_prompt(q, k, proj3):
    B, T, _ = q.shape
    L = MOBA_BLOCK
    nblk = T // L
    est = 2 * _nbytes((T, ATT_DH), F32) + 8 * _nbytes((L, T), F32)
    return pl.pallas_call(
        functools.partial(_moba_prompt_body, nblk=nblk),
        grid=(B, ATT_HEADS, nblk),
        in_specs=[pl.BlockSpec((1, L, ATT_DH), lambda b, h, i: (b, i, h)),
                  pl.BlockSpec((1, T, ATT_DH), lambda b, h, i: (b, 0, h)),
                  pl.BlockSpec((1, T, ATT_DH), lambda b, h, i: (b, 0, 2 * ATT_HEADS + h))],
        out_specs=pl.BlockSpec((1, L, ATT_DH), lambda b, h, i: (b, i, h)),
        out_shape=jax.ShapeDtypeStruct((B, T, ATT_HEADS * ATT_DH), BF16),
        compiler_params=_params(("parallel", "parallel", "arbitrary"), est),
    )(q, k, proj3)


def _kmean_body(pt_ref, kc_ref, o_ref, *, pages_per_blk, n_pages):
    p = pl.program_id(1)

    @pl.when(p == 0)
    def _():
        o_ref[...] = jnp.zeros(o_ref.shape, F32)

    s = jnp.sum(kc_ref[0, 0], axis=0, keepdims=True)
    j = p // pages_per_blk
    o_ref[0, pl.ds(j, 1), :] = o_ref[0, pl.ds(j, 1), :] + s

    @pl.when(p == n_pages - 1)
    def _():
        o_ref[...] = o_ref[...] * (1.0 / MOBA_BLOCK)


def _paged_block_means(cache_k4, layer, page_table):
    B, n_pages = page_table.shape
    page = cache_k4.shape[2]
    W = cache_k4.shape[3]
    ppb = MOBA_BLOCK // page
    nblk = n_pages // ppb
    grid_spec = pltpu.PrefetchScalarGridSpec(
        num_scalar_prefetch=1,
        grid=(B, n_pages),
        in_specs=[pl.BlockSpec((1, 1, page, W), lambda b, p, pt: (layer, pt[b * n_pages + p], 0, 0))],
        out_specs=pl.BlockSpec((1, nblk, W), lambda b, p, pt: (b, 0, 0)),
    )
    return pl.pallas_call(
        functools.partial(_kmean_body, pages_per_blk=ppb, n_pages=n_pages),
        grid_spec=grid_spec,
        out_shape=jax.ShapeDtypeStruct((B, nblk, W), F32),
        compiler_params=_params(("parallel", "arbitrary"), _nbytes((page, W), F32) + _nbytes((nblk, W), F32)),
    )(page_table.reshape(-1), cache_k4)


def _moba_pick_body(q_ref, km_ref, o_ref, *, nblk):
    big = jnp.float32(nblk)
    for h in range(ATT_HEADS):
        q = q_ref[0, :, h * ATT_DH:(h + 1) * ATT_DH]
        km = km_ref[0, :, h * ATT_DH:(h + 1) * ATT_DH]
        s = lax.dot_general(q, km, NT, precision=HI, preferred_element_type=F32)
        col = lax.broadcasted_iota(I32, s.shape, 1).astype(F32)
        for r in range(MOBA_TOPK):
            m = jnp.max(s, axis=-1, keepdims=True)
            idx = jnp.min(jnp.where(s == m, col, big), axis=-1, keepdims=True)
            o_ref[0, h, :, r:r + 1] = idx.astype(I32)
            s = jnp.where(col == idx, -jnp.inf, s)


def _moba_pick(q, kmean):
    B, Tq, W = q.shape
    nblk = kmean.shape[1]
    assert nblk >= MOBA_TOPK
    return pl.pallas_call(
        functools.partial(_moba_pick_body, nblk=nblk),
        grid=(B,),
        in_specs=[pl.BlockSpec((1, Tq, W), lambda b: (b, 0, 0)), pl.BlockSpec((1, nblk, W), lambda b: (b, 0, 0))],
        out_specs=pl.BlockSpec((1, ATT_HEADS, Tq, MOBA_TOPK), lambda b: (b, 0, 0, 0)),
        out_shape=jax.ShapeDtypeStruct((B, ATT_HEADS, Tq, MOBA_TOPK), I32),
        compiler_params=_params(("parallel",), _nbytes((nblk + Tq, W), F32)),
    )(q, kmean)


def _moba_sample_body(pt_ref, ix_ref, q_ref, kn_ref, vn_ref, kc_ref, vc_ref, o_ref, m_ref, l_ref, acc_ref, *, n_steps):
    t = pl.program_id(1)
    c = pl.program_id(3)
    Tq = q_ref.shape[1]
    scale = ATT_DH ** -0.5
    qb = jnp.broadcast_to(q_ref[0, pl.ds(t, 1), :], (Tq, ATT_DH))

    @pl.when(c == 0)
    def _():
        s = lax.dot_general(qb.astype(BF16), kn_ref[0].astype(BF16), NT, preferred_element_type=F32) * scale
        kpos = lax.broadcasted_iota(I32, s.shape, 1)
        s = jnp.where(kpos <= t, s, NEG)
        m = jnp.max(s, axis=-1, keepdims=True)
        p = jnp.exp(s - m)
        p = p.astype(BF16).astype(F32)
        vn = vn_ref[0].astype(BF16).astype(F32)
        acc = jnp.zeros((Tq, ATT_DH), F32)
        for j in range(Tq):
            acc = acc + p[:, j:j + 1] * vn[j:j + 1, :]
        m_ref[...] = m
        l_ref[...] = jnp.sum(jnp.exp(s - m), axis=-1, keepdims=True)
        acc_ref[...] = acc

    s = lax.dot_general(qb.astype(BF16), kc_ref[0, 0].astype(BF16), NT, preferred_element_type=F32) * scale
    m_old = m_ref[...]
    m_new = jnp.maximum(m_old, jnp.max(s, axis=-1, keepdims=True))
    alpha = jnp.exp(m_old - m_new)
    p = jnp.exp(s - m_new)
    l_ref[...] = alpha * l_ref[...] + jnp.sum(p, axis=-1, keepdims=True)
    acc_ref[...] = alpha * acc_ref[...] + jnp.dot(p.astype(BF16), vc_ref[0, 0].astype(BF16),
                                                  preferred_element_type=F32)
    m_ref[...] = m_new

    @pl.when(c == n_steps - 1)
    def _():
        o_ref[0] = (acc_ref[...] / l_ref[...])[0:1, :]


def _moba_sample(q, k_new, v_new, cache_k4, cache_v4, layer, page_table, picks, t_valid):
    B, Tq, W = q.shape
    n_pages = page_table.shape[1]
    page = cache_k4.shape[2]
    ppb = MOBA_BLOCK // page
    n_steps = MOBA_TOPK * ppb

    def page_of(b, t, h, c, pt, ix):
        blk = ix[((b * ATT_HEADS + h) * Tq + t) * MOBA_TOPK + c // ppb]
        return pt[b * n_pages + blk * ppb + c % ppb]

    cache_spec = pl.BlockSpec((1, 1, page, ATT_DH), lambda b, t, h, c, pt, ix: (layer, page_of(b, t, h, c, pt, ix), 0, h))
    new_spec = pl.BlockSpec((1, Tq, ATT_DH), lambda b, t, h, c, pt, ix: (b, 0, h))
    grid_spec = pltpu.PrefetchScalarGridSpec(
        num_scalar_prefetch=2,
        grid=(B, t_valid, ATT_HEADS, n_steps),
        in_specs=[new_spec, new_spec, new_spec, cache_spec, cache_spec],
        out_specs=pl.BlockSpec((1, 1, ATT_DH), lambda b, t, h, c, pt, ix: ((b * t_valid + t) * ATT_HEADS + h, 0, 0)),
        scratch_shapes=[pltpu.VMEM((Tq, 1), F32), pltpu.VMEM((Tq, 1), F32), pltpu.VMEM((Tq, ATT_DH), F32)],
    )
    out = pl.pallas_call(
        functools.partial(_moba_sample_body, n_steps=n_steps),
        grid_spec=grid_spec,
        out_shape=jax.ShapeDtypeStruct((B * t_valid * ATT_HEADS, 1, ATT_DH), F32),
        compiler_params=_params(("parallel", "parallel", "parallel", "arbitrary"), 4 * _nbytes((page, ATT_DH), F32)),
    )(page_table.reshape(-1), picks.reshape(-1), q, k_new, v_new, cache_k4, cache_v4)
    return out.reshape(B, t_valid, W)


def _conv_body(a_ref, g_ref, buf_ref, w_ref, b_ref, lg_ref, lb_ref, o_ref, st_ref, ext_ref, *, tt, nt, tv_last):
    ti = pl.program_id(1)

    @pl.when(ti == 0)
    def _():
        ext_ref[0:CONV_PAD, :] = buf_ref[0]

    @pl.when(ti > 0)
    def _():
        ext_ref[0:CONV_PAD, :] = ext_ref[tt:tt + CONV_PAD, :]

    ext_ref[CONV_PAD:CONV_PAD + tt, :] = a_ref[0] * _sigmoid(g_ref[0])
    off = CONV_PAD - (CONV_K - 1)
    acc = ext_ref[pl.ds(off, tt), :] * w_ref[0:1, :]
    for k in range(1, CONV_K):
        acc = acc + ext_ref[pl.ds(off + k, tt), :] * w_ref[k:k + 1, :]
    y = acc + b_ref[...]
    mu = jnp.mean(y, axis=-1, keepdims=True)
    yc = y - mu
    var = jnp.mean(yc * yc, axis=-1, keepdims=True)
    yn = yc * lax.rsqrt(var + EPS) * lg_ref[...] + lb_ref[...]
    o_ref[0] = _silu(yn).astype(o_ref.dtype)

    @pl.when(ti == nt - 1)
    def _():
        st_ref[0] = ext_ref[pl.ds(tv_last + off, CONV_K - 1), :]


def _conformer_conv(proj3, buf, w, b, ln_g, ln_b, t_valid, *, tt=256):
    B, T, _ = proj3.shape
    C = BR_W
    tt = _tile(T, tt, SUBLANE)
    nt = T // tt
    tv_last = t_valid - (nt - 1) * tt
    assert 0 < tv_last <= tt and (nt == 1 or tt >= CONV_PAD)
    buf32 = jnp.pad(buf, ((0, 0), (CONV_PAD - (CONV_K - 1), 0), (0, 0)))
    vec = lambda x: x.reshape(1, C)
    est = 3 * _nbytes((tt, C), F32) + 4 * _nbytes((tt + CONV_PAD, C), F32)
    return pl.pallas_call(
        functools.partial(_conv_body, tt=tt, nt=nt, tv_last=tv_last),
        grid=(B, nt),
        in_specs=[pl.BlockSpec((1, tt, C), lambda bi, ti: (bi, ti, 3)),
                  pl.BlockSpec((1, tt, C), lambda bi, ti: (bi, ti, 4)),
                  pl.BlockSpec((1, CONV_PAD, C), lambda bi, ti: (bi, 0, 0)),
                  pl.BlockSpec((CONV_K, C), lambda bi, ti: (0, 0)),
                  pl.BlockSpec((1, C), lambda bi, ti: (0, 0)),
                  pl.BlockSpec((1, C), lambda bi, ti: (0, 0)),
                  pl.BlockSpec((1, C), lambda bi, ti: (0, 0))],
        out_specs=[pl.BlockSpec((1, tt, C), lambda bi, ti: (bi, ti, 0)),
                   pl.BlockSpec((1, CONV_K - 1, C), lambda bi, ti: (bi, 0, 0))],
        out_shape=[jax.ShapeDtypeStruct((B, T, C), BF16), jax.ShapeDtypeStruct((B, CONV_K - 1, C), F32)],
        scratch_shapes=[pltpu.VMEM((tt + CONV_PAD, C), F32)],
        compiler_params=_params(("parallel", "arbitrary"), est),
    )(proj3, proj3, buf32, w, vec(b), vec(ln_g), vec(ln_b))


def _hgrn_body(f_ref, i_ref, q_ref, g_ref, s0_ref, lb_ref, ng_ref, o_ref, sn_ref, st_ref, *, c, nc, tv_last):
    ci = pl.program_id(2)

    @pl.when(ci == 0)
    def _():
        st_ref[...] = s0_ref[0, 0].T

    lb = lb_ref[...]
    f = lb + (1.0 - lb) * _sigmoid(f_ref[0])
    logf = jnp.log(f)
    kk = 1.0 - f
    if tv_last < c:
        valid = lax.broadcasted_iota(I32, (c, HG_DK), 0) < tv_last
        logf = jnp.where(valid, logf, 0.0)
        kk = jnp.where(valid, kk, 0.0)
    q = _silu(q_ref[0])
    v = i_ref[0]
    tri = jnp.where(lax.broadcasted_iota(I32, (c, c), 0) >= lax.broadcasted_iota(I32, (c, c), 1), 1.0, 0.0)
    bcum = jnp.dot(tri, logf, precision=HI, preferred_element_type=F32)

    nrb = c // SUBLANE
    rows = lax.broadcasted_iota(I32, (SUBLANE, HG_DK), 0)
    rows1 = lax.broadcasted_iota(I32, (SUBLANE, 1), 0)
    qb =[q[r * SUBLANE:(r + 1) * SUBLANE] for r in range(nrb)]
    bb = [bcum[r * SUBLANE:(r + 1) * SUBLANE] for r in range(nrb)]
    ob = [jnp.zeros((SUBLANE, HG_DK), F32) for _ in range(nrb)]
    for s in range(c):
        bs = bcum[s:s + 1, :]
        ks = kk[s:s + 1, :]
        vs = v[s:s + 1, :]
        r0 = s // SUBLANE
        for r in range(r0, nrb):
            d = bb[r] - bs
            if r == r0:
                d = jnp.where(rows >= (s - r0 * SUBLANE), d, 0.0)
            a = jnp.sum(qb[r] * ks * jnp.exp(d), axis=-1, keepdims=True)
            if r == r0:
                a = jnp.where(rows1 >= (s - r0 * SUBLANE), a, 0.0)
            ob[r] = ob[r] + a * vs
    o = jnp.concatenate(ob, axis=0) if nrb > 1 else ob[0]

    st = st_ref[...]
    o = o + lax.dot_general(q * jnp.exp(bcum), st, NT, precision=HI, preferred_element_type=F32)
    b_last = bcum[c - 1:c, :]
    kd = kk * jnp.exp(b_last - bcum)
    st_new = st * jnp.exp(b_last) + lax.dot_general(v, kd, TN, precision=HI, preferred_element_type=F32)
    st_ref[...] = st_new

    o = o * lax.rsqrt(jnp.mean(o * o, axis=-1, keepdims=True) + EPS) * ng_ref[...]
    o_ref[0] = (o * _silu(g_ref[0])).astype(o_ref.dtype)

    @pl.when(ci == nc - 1)
    def _():
        sn_ref[0, 0] = st_new.T


def _hgrn2(proj3, s0, lb, norm_g, t_valid):
    B, T, _ = proj3.shape
    c = HG_CHUNK if T % HG_CHUNK == 0 else T
    nc = T // c
    tv_last = t_valid - (nc - 1) * c
    assert 0 < tv_last <= c and c % SUBLANE == 0
    base = 5 * BR_W // HG_DK
    nh = HG_HEADS

    def col(group):
        return pl.BlockSpec((1, c, HG_DK), lambda b, h, ci: (b, ci, base + group * nh + h))

    vec = pl.BlockSpec((1, HG_DK), lambda b, h, ci: (0, h))
    state = pl.BlockSpec((1, 1, HG_DK, HG_DK), lambda b, h, ci: (b, h, 0, 0))
    return pl.pallas_call(
        functools.partial(_hgrn_body, c=c, nc=nc, tv_last=tv_last),
        grid=(B, nh, nc),
        in_specs=[col(0), col(1), col(2), col(3), state, vec, vec],
        out_specs=[pl.BlockSpec((1, c, HG_DK), lambda b, h, ci: (b, ci, h)), state],
        out_shape=[jax.ShapeDtypeStruct((B, T, nh * HG_DK), BF16), jax.ShapeDtypeStruct((B, nh, HG_DK, HG_DK), F32)],
        scratch_shapes=[pltpu.VMEM((HG_DK, HG_DK), F32)],
        compiler_params=_params(("parallel", "parallel", "arbitrary"), 16 * _nbytes((HG_DK, HG_DK), F32)),
    )(proj3, proj3, proj3, proj3, s0, lb.reshape(1, -1), norm_g.reshape(1, -1))


def _xattn_body(q_ref, k_ref, v_ref, g_ref, o_ref):
    x = q_ref[0]
    q = x * lax.rsqrt(jnp.mean(x * x, axis=-1, keepdims=True) + EPS) * g_ref[...]
    s = lax.dot_general(q.astype(BF16), k_ref[0].astype(BF16), NT, preferred_element_type=F32) * (X_DH ** -0.5)
    p = jnp.exp(s - jnp.max(s, axis=-1, keepdims=True))
    l = jnp.sum(p, axis=-1, keepdims=True)
    o = jnp.dot(p.astype(BF16), v_ref[0].astype(BF16), preferred_element_type=F32)
    o_ref[0] = (o / l).astype(o_ref.dtype)


def _cross_attention(proj3, mem_k, mem_v, g, *, tq=512):
    B, T, _ = proj3.shape
    M = mem_k.shape[1]
    tq = _tile(T, tq, SUBLANE)
    base = 9 * BR_W // X_DH
    mem = pl.BlockSpec((1, M, X_DH), lambda b, h, i: (b, 0, h))
    return pl.pallas_call(
        _xattn_body,
        grid=(B, X_HEADS, T // tq),
        in_specs=[pl.BlockSpec((1, tq, X_DH), lambda b, h, i: (b, i, base + h)), mem, mem,
                  pl.BlockSpec((1, X_DH), lambda b, h, i: (0, 0))],
        out_specs=pl.BlockSpec((1, tq, X_DH), lambda b, h, i: (b, i, h)),
        out_shape=jax.ShapeDtypeStruct((B, T, X_HEADS * X_DH), BF16),
        compiler_params=_params(("parallel", "parallel", "arbitrary"), 6 * _nbytes((tq, X_DH), F32)),
    )(proj3, mem_k, mem_v, g.reshape(1, X_DH))


def _merge_body(gz_ref, oa_ref, oc_ref, oh_ref, ox_ref, wg_ref, wb_ref, o_ref):
    gz = gz_ref[...].astype(BF16)
    acc = None
    for bi, ob_ref in enumerate((oa_ref, oc_ref, oh_ref, ox_ref)):
        gate = _sigmoid(jnp.dot(gz, wg_ref[bi], preferred_element_type=F32))
        term = gate * jnp.dot(ob_ref[...], wb_ref[bi], preferred_element_type=F32)
        acc = term if acc is None else acc + term
    o_ref[...] = acc.astype(o_ref.dtype)


def _gated_merge(proj, branches, w_gate, w_branch, *, tm=1024, tn=512):
    M = proj.shape[0]
    nb, R, D = w_gate.shape
    W = w_branch.shape[1]
    tm = _tile(M, tm, 16)
    tn = _tile(D, tn, LANE)
    gz_blk = (proj.shape[1] - R) // R
    ob_spec = pl.BlockSpec((tm, W), lambda i, j: (i, 0))
    est = (_nbytes((tm, R), F32) + nb * _nbytes((tm, W), BF16) + nb * _nbytes((R + W, tn), BF16)
           + 4 * _nbytes((tm, tn), F32))
    return pl.pallas_call(
        _merge_body,
        grid=(M // tm, D // tn),
        in_specs=[pl.BlockSpec((tm, R), lambda i, j: (i, gz_blk)), ob_spec, ob_spec, ob_spec, ob_spec,
                  pl.BlockSpec((nb, R, tn), lambda i, j: (0, 0, j)),
                  pl.BlockSpec((nb, W, tn), lambda i, j: (0, 0, j))],
        out_specs=pl.BlockSpec((tm, tn), lambda i, j: (i, j)),
        out_shape=jax.ShapeDtypeStruct((M, D), BF16),
        compiler_params=_params(("parallel", "arbitrary"), est),
    )(proj, *branches, w_gate, w_branch)


def _swiglu_up_body(h_ref, w1_ref, w3_ref, o_ref):
    h = h_ref[...]
    a = jnp.dot(h, w1_ref[...], preferred_element_type=F32)
    b = jnp.dot(h, w3_ref[...], preferred_element_type=F32)
    o_ref[...] = (_silu(a) * b).astype(o_ref.dtype)


def _swiglu_up(h, w1, w3, *, tm=1024, tn=256):
    M, D = h.shape
    Fd = w1.shape[1]
    tm = _tile(M, tm, 16)
    tn = _tile(Fd, tn, LANE)
    est = _nbytes((tm, D), BF16) + 2 * _nbytes((D, tn), BF16) + 4 * _nbytes((tm, tn), F32)
    return pl.pallas_call(
        _swiglu_up_body,
        grid=(M // tm, Fd // tn),
        in_specs=[pl.BlockSpec((tm, D), lambda i, j: (i, 0)),
                  pl.BlockSpec((D, tn), lambda i, j: (0, j)),
                  pl.BlockSpec((D, tn), lambda i, j: (0, j))],
        out_specs=pl.BlockSpec((tm, tn), lambda i, j: (i, j)),
        out_shape=jax.ShapeDtypeStruct((M, Fd), BF16),
        compiler_params=_params(("parallel", "arbitrary"), est),
    )(h, w1, w3)


def _gather_body(idx_ref, src_ref, o_ref, sem, *, tg):
    def row_copy(r, src_row):
        return pltpu.make_async_copy(src_ref.at[pl.ds(src_row, 1)], o_ref.at[pl.ds(r, 1)], sem)

    def issue(r, carry):
        row_copy(r, idx_ref[0, 0, r]).start()
        return carry

    def drain(r, carry):
        row_copy(r, 0).wait()
        return carry

    lax.fori_loop(0, tg, issue, 0)
    lax.fori_loop(0, tg, drain, 0)


def _gather_rows(src, idx, *, tg=128):
    R = idx.shape[0]
    W = src.shape[1]
    tg = _tile(R, tg, SUBLANE)
    return pl.pallas_call(
        functools.partial(_gather_body, tg=tg),
        grid=(R // tg,),
        in_specs=[pl.BlockSpec((1, 1, tg), lambda i: (i, 0, 0), memory_space=pltpu.SMEM),
                  pl.BlockSpec(memory_space=pl.ANY)],
        out_specs=pl.BlockSpec((tg, W), lambda i: (i, 0)),
        out_shape=jax.ShapeDtypeStruct((R, W), src.dtype),
        scratch_shapes=[pltpu.SemaphoreType.DMA(())],
        compiler_params=_params(("arbitrary",), _nbytes((tg, W), src.dtype)),
    )(idx.reshape(R // tg, 1, tg), src)


def _moe_up_body(te_ref, nu_ref, x_ref, w1_ref, w3_ref, o_ref):
    t = pl.program_id(0)

    @pl.when(t < nu_ref[0])
    def _():
        x = x_ref[...].astype(BF16)
        a = jnp.dot(x, w1_ref[0], preferred_element_type=F32)
        b = jnp.dot(x, w3_ref[0], preferred_element_type=F32)
        o_ref[...] = (_silu(a) * b).astype(o_ref.dtype)

    @pl.when(t >= nu_ref[0])
    def _():
        o_ref[...] = jnp.zeros(o_ref.shape, o_ref.dtype)


def _moe_up(xs, w1, w3, tile_e, n_used, *, tm, tn=256):
    R, D = xs.shape
    Fe = w1.shape[2]
    nt = R // tm
    nf = pl.cdiv(Fe, tn)

    def w_map(t, j, te, nu):
        live = t < nu[0]
        return (te[jnp.minimum(t, nu[0] - 1)], 0, jnp.where(live, j, nf - 1))

    grid_spec = pltpu.PrefetchScalarGridSpec(
        num_scalar_prefetch=2,
        grid=(nt, nf),
        in_specs=[pl.BlockSpec((tm, D), lambda t, j, te, nu: (t, 0)),
                  pl.BlockSpec((1, D, tn), w_map),
                  pl.BlockSpec((1, D, tn), w_map)],
        out_specs=pl.BlockSpec((tm, tn), lambda t, j, te, nu: (t, j)),
    )
    est = _nbytes((tm, D), F32) + _nbytes((tm, D), BF16) + 2 * _nbytes((D, tn), BF16) + 4 * _nbytes((tm, tn), F32)
    return pl.pallas_call(
        _moe_up_body,
        grid_spec=grid_spec,
        out_shape=jax.ShapeDtypeStruct((R, Fe), BF16),
        compiler_params=_params(("arbitrary", "arbitrary"), est),
    )(tile_e, n_used, xs, w1, w3)


def _moe_down_body(te_ref, nu_ref, x_ref, w_ref, g_ref, o_ref):
    t = pl.program_id(0)

    @pl.when(t < nu_ref[0])
    def _():
        o_ref[...] = jnp.dot(x_ref[...], w_ref[0], preferred_element_type=F32) * g_ref[...]

    @pl.when(t >= nu_ref[0])
    def _():
        o_ref[...] = jnp.zeros(o_ref.shape, o_ref.dtype)


def _moe_down(g, w2, row_gate, tile_e, n_used, *, tm, tn=512):
    R, Fe = g.shape
    D = w2.shape[2]
    nt = R // tm
    tn = _tile(D, tn, LANE)
    nn = D // tn

    def w_map(t, j, te, nu):
        live = t < nu[0]
        return (te[jnp.minimum(t, nu[0] - 1)], 0, jnp.where(live, j, nn - 1))

    grid_spec = pltpu.PrefetchScalarGridSpec(
        num_scalar_prefetch=2,
        grid=(nt, nn),
        in_specs=[pl.BlockSpec((tm, Fe), lambda t, j, te, nu: (t, 0)),
                  pl.BlockSpec((1, Fe, tn), w_map),
                  pl.BlockSpec((tm, 1), lambda t, j, te, nu: (t, 0))],
        out_specs=pl.BlockSpec((tm, tn), lambda t, j, te, nu: (t, j)),
    )
    est = _nbytes((tm, Fe), BF16) + _nbytes((Fe, tn), BF16) + 3 * _nbytes((tm, tn), F32) + _nbytes((tm, LANE), F32)
    return pl.pallas_call(
        _moe_down_body,
        grid_spec=grid_spec,
        out_shape=jax.ShapeDtypeStruct((R, D), F32),
        compiler_params=_params(("arbitrary", "arbitrary"), est),
    )(tile_e, n_used, g, w2, row_gate)


def _add3_body(a_ref, b_ref, c_ref, o_ref):
    o_ref[...] = a_ref[...] + (b_ref[...] + c_ref[...])


def _add3(a, b, c, *, tm=256):
    M, D = a.shape
    tm = _tile(M, tm, SUBLANE)
    spec = pl.BlockSpec((tm, D), lambda i: (i, 0))
    return pl.pallas_call(
        _add3_body, grid=(M // tm,), in_specs=[spec, spec, spec], out_specs=spec,
        out_shape=jax.ShapeDtypeStruct((M, D), F32),
        compiler_params=_params(("parallel",), 4 * _nbytes((tm, D), F32)),
    )(a, b, c)


def _moe_ffn(x, norm_g, w_router, w1, w3, w2):
    T, D = x.shape
    A = T * TOP_K
    tm = 512 if A >= 8192 else 64
    h, logits = _rmsnorm(x, norm_g, w_router, keep_f32=True)
    top_logit, top_e = lax.top_k(logits, TOP_K)
    gate = jax.nn.softmax(top_logit, axis=-1)
    e_flat = top_e.reshape(A).astype(I32)
    tok_flat = jnp.repeat(jnp.arange(T, dtype=I32), TOP_K)
    g_flat = gate.reshape(A)
    order = jnp.argsort(e_flat)
    e_s, tok_s, g_s = e_flat[order], tok_flat[order], g_flat[order]
    counts = jnp.bincount(e_flat, length=N_EXPERTS).astype(I32)
    tiles_per_e = (counts + tm - 1) // tm
    tile_end = jnp.cumsum(tiles_per_e).astype(I32)
    tile_start = tile_end - tiles_per_e
    row_start = (jnp.cumsum(counts) - counts).astype(I32)
    dest = tile_start[e_s] * tm + (jnp.arange(A, dtype=I32) - row_start[e_s])
    n_tiles = A // tm + N_EXPERTS
    rows_tok = jnp.zeros(n_tiles * tm, I32).at[dest].set(tok_s)
    rows_gate = jnp.zeros(n_tiles * tm, F32).at[dest].set(g_s)
    tile_e = jnp.minimum(jnp.searchsorted(tile_end, jnp.arange(n_tiles, dtype=I32), side='right'),
                         N_EXPERTS - 1).astype(I32)
    n_used = tile_end[-1:].astype(I32)
    slot_row = jnp.zeros(A, I32).at[order].set(dest).reshape(T, TOP_K)

    xs = _gather_rows(h, rows_tok)
    mid = _moe_up(xs, w1, w3, tile_e, n_used, tm=tm)
    ys = _moe_down(mid, w2, rows_gate.reshape(-1, 1), tile_e, n_used, tm=tm)
    y0 = _gather_rows(ys, slot_row[:, 0])
    y1 = _gather_rows(ys, slot_row[:, 1])
    return _add3(x, y0, y1)


def _token_mix(x, B, T, t_valid, pos0, lw, mem_k, mem_v, conv_buf, s0, past):
    M = B * T
    h = _rmsnorm(x, lw['norm_mix_g'])
    proj = _matmul(h, lw['w_in'])
    proj3 = proj.reshape(B, T, -1)
    cos, sin = _rope_tables(pos0 + jnp.arange(T, dtype=I32))
    if M // T > 1 and T < 256:
        cos, sin = jnp.tile(cos, (B, 1)), jnp.tile(sin, (B, 1))
    q, k = _qk_prep(proj, cos, sin, lw['q_norm_g'], lw['k_norm_g'])
    q3 = q.reshape(B, T, -1)
    k3 = k.reshape(B, T, -1)
    v3 = proj3[:, :, 2 * BR_W:3 * BR_W]
    if past is None:
        o_att = _moba_prompt(q3, k3, proj3)
    else:
        cache_k4, cache_v4, layer, page_table = past
        kmean = _paged_block_means(cache_k4, layer, page_table)
        picks = _moba_pick(q3, kmean)
        o_valid = _moba_sample(q3, k3, v3, cache_k4, cache_v4, layer, page_table, picks, t_valid)
        o_att = jnp.pad(o_valid, ((0, 0), (0, T - t_valid), (0, 0))).astype(BF16)
    o_conv, conv_new = _conformer_conv(proj3, conv_buf, lw['conv_w'], lw['conv_b'], lw['conv_ln_g'],
                                       lw['conv_ln_b'], t_valid)
    o_hg, s_new = _hgrn2(proj3, s0, lw['lb'], lw['hg_norm_g'], t_valid)
    o_x = _cross_attention(proj3, mem_k, mem_v, lw['xq_norm_g'])
    branches = [o.reshape(M, BR_W) for o in (o_att, o_conv, o_hg, o_x)]
    merged = _gated_merge(proj, branches, lw['w_gate'], lw['w_branch'])
    x = _matmul(merged, lw['w_out'], res=x)
    return x, k3, v3, conv_new, s_new


def _memory_kv(mem2, B, lw):
    hm = _rmsnorm(mem2, lw['norm_mem_g'])
    kv = _matmul(hm, lw['w_mem_kv'])
    mk = _headnorm(kv, lw['xk_norm_g'], heads=X_HEADS, dh=X_DH)
    mv = kv[:, X_HEADS * X_DH:]
    return mk.reshape(B, -1, X_HEADS * X_DH), mv.reshape(B, -1, X_HEADS * X_DH)


def _dense_ffn(x, norm_g, w1, w3, w2):
    h = _rmsnorm(x, norm_g)
    mid = _swiglu_up(h, w1, w3)
    return _matmul(mid, w2, res=x, tm=512, tn=256)


def kernel(x_prompt, x_sample, cache_k, cache_v, page_table, cache_mem_k, cache_mem_v, state_conv, state_hgrn, mem_prompt, norm_mix_g, norm_mem_g, norm_ffn_g, w_in, attn_q_norm_g, attn_k_norm_g, conv_w, conv_b, conv_ln_g, conv_ln_b, hgrn_lb, hgrn_norm_g, w_mem_kv, xq_norm_g, xk_norm_g, w_gate, w_branch, w_out, dense_w1, dense_w3, dense_w2, moe_router, moe_w1, moe_w3, moe_w2):
    bp, tp, D = x_prompt.shape
    bs, ts, _ = x_sample.shape
    depth = w_in.shape[0]
    n_pool, page = cache_k.shape[1], cache_k.shape[2]
    past_len = page_table.shape[1] * page
    ts_pad = -(-ts // SUBLANE) * SUBLANE

    p_lb = jax.nn.softmax(hgrn_lb.astype(F32), axis=0)
    lower_bounds = jnp.concatenate([jnp.zeros_like(p_lb[:1]), jnp.cumsum(p_lb[1:], axis=0)], axis=0)

    xp = x_prompt.reshape(bp * tp, D)
    xs = jnp.pad(x_sample, ((0, 0), (0, ts_pad - ts), (0, 0))).reshape(bs * ts_pad, D)
    mem2 = mem_prompt.reshape(-1, D)
    cache_k4 = cache_k.reshape(depth, n_pool, page, -1)
    cache_v4 = cache_v.reshape(depth, n_pool, page, -1)
    zero_conv = jnp.zeros((bp, CONV_K - 1, BR_W), F32)
    zero_state = jnp.zeros((bp, HG_HEADS, HG_DK, HG_DK), F32)

    outs = [[] for _ in range(10)]
    for l in range(depth):
        lw = dict(norm_mix_g=norm_mix_g[l], norm_mem_g=norm_mem_g[l], w_in=w_in[l].astype(BF16),
                  q_norm_g=attn_q_norm_g[l], k_norm_g=attn_k_norm_g[l], conv_w=conv_w[l], conv_b=conv_b[l],
                  conv_ln_g=conv_ln_g[l], conv_ln_b=conv_ln_b[l], lb=lower_bounds[l], hg_norm_g=hgrn_norm_g[l],
                  w_mem_kv=w_mem_kv[l].astype(BF16), xq_norm_g=xq_norm_g[l], xk_norm_g=xk_norm_g[l],
                  w_gate=w_gate[l].astype(BF16), w_branch=w_branch[l].astype(BF16), w_out=w_out[l].astype(BF16))
        mk_p, mv_p = _memory_kv(mem2, bp, lw)
        xp, k_p, v_p, c_p, s_p = _token_mix(xp, bp, tp, tp, 0, lw, mk_p, mv_p, zero_conv, zero_state, None)
        mem_ks = cache_mem_k[l].reshape(bs, -1, X_HEADS * X_DH)
        mem_vs = cache_mem_v[l].reshape(bs, -1, X_HEADS * X_DH)
        xs, k_s, v_s, c_s, s_s = _token_mix(xs, bs, ts_pad, ts, past_len, lw, mem_ks, mem_vs, state_conv[l],
                                            state_hgrn[l], (cache_k4, cache_v4, l, page_table))
        i = l // 2
        if l % 2 == 0:
            w1, w3, w2 = dense_w1[i].astype(BF16), dense_w3[i].astype(BF16), dense_w2[i].astype(BF16)
            xp = _dense_ffn(xp, norm_ffn_g[l], w1, w3, w2)
            xs = _dense_ffn(xs, norm_ffn_g[l], w1, w3, w2)
        else:
            fpad = -moe_w1.shape[-1] % 512
            w1 = jnp.pad(moe_w1[i].astype(BF16), ((0, 0), (0, 0), (0, fpad)))
            w3 = jnp.pad(moe_w3[i].astype(BF16), ((0, 0), (0, 0), (0, fpad)))
            w2 = jnp.pad(moe_w2[i].astype(BF16), ((0, 0), (0, fpad), (0, 0)))
            xp = _moe_ffn(xp, norm_ffn_g[l], moe_router[i], w1, w3, w2)
            xs = _moe_ffn(xs, norm_ffn_g[l], moe_router[i], w1, w3, w2)
        hd = (ATT_HEADS, ATT_DH)
        new = (k_p.reshape(bp, tp, *hd), v_p.reshape(bp, tp, *hd),
               k_s[:, :ts].reshape(bs, ts, *hd), v_s[:, :ts].reshape(bs, ts, *hd),
               mk_p.reshape(bp, -1, X_HEADS, X_DH), mv_p.reshape(bp, -1, X_HEADS, X_DH),
               c_p, c_s, s_p, s_s)
        for acc, val in zip(outs, new):
            acc.append(val)
    y_p = xp.reshape(bp, tp, D)
    y_s = xs.reshape(bs, ts_pad, D)[:, :ts]
    return (y_p, y_s) + tuple(jnp.stack(o) for o in outs)
```
